```python
import jax, jax.numpy as jnp
from jax import lax
import numpy as np

D_MODEL = 1024
BATCH = 2
SEQ = 8192
DEPTH = 4

HEAD_DIM = 64
N_Q_HEADS = D_MODEL // (2 * HEAD_DIM)
N_KV_HEADS = max(1, N_Q_HEADS // 4)
GQA_GROUP = N_Q_HEADS // N_KV_HEADS
WINDOW = 128
ATTN_BLOCK = 128
GM_HEADS = N_Q_HEADS
GM_HEAD_DIM = HEAD_DIM
CHUNK = 128
ATTN_W = N_Q_HEADS * HEAD_DIM
KV_W = N_KV_HEADS * HEAD_DIM
GM_W = GM_HEADS * GM_HEAD_DIM
D_MIX = ATTN_W + GM_W
D_IN = ATTN_W + 2 * KV_W + 2 * GM_W
D_FF = ((8 * D_MODEL + 3 * 256 - 1) // (3 * 256)) * 256
PLE_DIM = 256
NORM_EPS = 1e-6
NEG_BIG = -1e30

kernel_name = "hymba_swa_gmlp_hybrid"


def rmsnorm(x, g):
    xf = x.astype(jnp.float32)
    y = xf * lax.rsqrt(jnp.mean(xf * xf, axis=-1, keepdims=True) + NORM_EPS)
    return (y * g.astype(jnp.float32)).astype(x.dtype)


def layernorm(x, g, b):
    xf = x.astype(jnp.float32)
    mu = jnp.mean(xf, axis=-1, keepdims=True)
    xc = xf - mu
    y = xc * lax.rsqrt(jnp.mean(xc * xc, axis=-1, keepdims=True) + NORM_EPS)
    return (y * g.astype(jnp.float32) + b.astype(jnp.float32)).astype(x.dtype)


def alibi_slopes(n_heads):
    return jnp.exp2(-8.0 * (jnp.arange(n_heads, dtype=jnp.float32) + 1.0) / n_heads)


def sliding_window_attention(q, k, v, sinks):
    B, S = q.shape[0], q.shape[1]
    nb = S // ATTN_BLOCK
    qb = q.reshape(B, nb, ATTN_BLOCK, N_KV_HEADS, GQA_GROUP, HEAD_DIM)
    kb = k.reshape(B, nb, ATTN_BLOCK, N_KV_HEADS, HEAD_DIM)
    vb = v.reshape(B, nb, ATTN_BLOCK, N_KV_HEADS, HEAD_DIM)
    pad = ((0, 0), (1, 0), (0, 0), (0, 0), (0, 0))
    kk = jnp.concatenate([jnp.pad(kb[:, :-1], pad), kb], axis=2)
    vv = jnp.concatenate([jnp.pad(vb[:, :-1], pad), vb], axis=2)
    scale = HEAD_DIM ** -0.5
    s = jnp.einsum('bnqkgd,bnskd->bnkgqs', qb, kk,
                   preferred_element_type=jnp.float32) * scale
    qi = jnp.arange(ATTN_BLOCK)[:, None]
    kj = jnp.arange(2 * ATTN_BLOCK)[None, :]
    dist = qi + ATTN_BLOCK - kj
    band = (dist >= 0) & (dist < WINDOW)
    blk = jnp.arange(nb)[:, None, None]
    valid = band[None] & ((blk > 0) | (kj >= ATTN_BLOCK)[None])
    slopes = alibi_slopes(N_Q_HEADS).reshape(N_KV_HEADS, GQA_GROUP)
    bias = -slopes[:, :, None, None] * dist.astype(jnp.float32)
    s = jnp.where(valid[None, :, None, None], s + bias[None, None], NEG_BIG)
    sink = sinks.astype(jnp.float32).reshape(N_KV_HEADS, GQA_GROUP)[None, None, :, :, None, None]
    m = jnp.maximum(jnp.max(s, axis=-1, keepdims=True), sink)
    e = jnp.exp(s - m)
    pr = e / (jnp.sum(e, axis=-1, keepdims=True) + jnp.exp(sink - m))
    o = jnp.einsum('bnkgqs,bnskd->bnqkgd', pr.astype(vv.dtype), vv)
    return o.reshape(B, S, ATTN_W)


def chunked_spatial_gating(zu, zv, ln_g, ln_b, ws, bs):
    B, S = zu.shape[0], zu.shape[1]
    nc = S // CHUNK
    zv = layernorm(zv, ln_g, ln_b)
    vh = zv.reshape(B, nc, CHUNK, GM_HEADS, GM_HEAD_DIM)
    causal = jnp.tril(jnp.ones((CHUNK, CHUNK), dtype=bool))
    w = jnp.where(causal[None], ws, jnp.zeros_like(ws))
    mixed = jnp.einsum('hts,bnshc->bnthc', w, vh) + bs.T[None, None, :, :, None]
    return zu * mixed.reshape(B, S, GM_W)


def setup_inputs(seed: int = 0) -> dict:
    key = jax.random.key(seed)
    ks = jax.random.split(key, 24)
    f32 = jnp.float32
    nrm = lambda k, shape, s: jax.random.normal(k, shape, f32) * s
    gain = lambda k, n: 1.0 + 0.05 * jax.random.normal(k, (DEPTH, n), f32)
    return {
        "x": nrm(ks[0], (BATCH, SEQ, D_MODEL), 1.0),
        "p": nrm(ks[1], (DEPTH, BATCH, SEQ, PLE_DIM), 1.0),
        "ln_mix_pre": gain(ks[2], D_MODEL),
        "w_in": nrm(ks[3], (DEPTH, D_MODEL, D_IN), D_MODEL ** -0.5),
        "attn_sinks": nrm(ks[4], (DEPTH, N_Q_HEADS), 1.0),
        "gm_ln_g": gain(ks[5], GM_W),
        "gm_ln_b": nrm(ks[6], (DEPTH, GM_W), 0.01),
        "gm_ws": nrm(ks[7], (DEPTH, GM_HEADS, CHUNK, CHUNK), CHUNK ** -0.5),
        "gm_bs": 1.0 + nrm(ks[8], (DEPTH, GM_HEADS, CHUNK), 0.01),
        "g_attn_out": gain(ks[9], ATTN_W),
        "g_gm_out": gain(ks[10], GM_W),
        "w_out": nrm(ks[11], (DEPTH, D_MIX, D_MODEL), D_MIX ** -0.5),
        "ln_mix_post": gain(ks[12], D_MODEL),
        "ln_ffn_pre": gain(ks[13], D_MODEL),
        "w_ffn_gate": nrm(ks[14], (DEPTH, D_MODEL, D_FF), D_MODEL ** -0.5),
        "w_ffn_up": nrm(ks[15], (DEPTH, D_MODEL, D_FF), D_MODEL ** -0.5),
        "w_ffn_down": nrm(ks[16], (DEPTH, D_FF, D_MODEL), D_FF ** -0.5),
        "ln_ffn_post": gain(ks[17], D_MODEL),
        "w_ple": nrm(ks[18], (DEPTH, PLE_DIM, D_MODEL), PLE_DIM ** -0.5),
        "ln_ple_gate": gain(ks[19], D_MODEL),
        "w_ple_gate": nrm(ks[20], (DEPTH, D_MODEL, D_MODEL), D_MODEL ** -0.5),
    }


def reference(x, p, ln_mix_pre, w_in, attn_sinks, gm_ln_g, gm_ln_b, gm_ws, gm_bs,
              g_attn_out, g_gm_out, w_out, ln_mix_post, ln_ffn_pre, w_ffn_gate,
              w_ffn_up, w_ffn_down, ln_ffn_post, w_ple, ln_ple_gate, w_ple_gate):
    h = x
    splits = [ATTN_W, ATTN_W + KV_W, ATTN_W + 2 * KV_W, ATTN_W + 2 * KV_W + GM_W]
    for i in range(DEPTH):
        a = rmsnorm(h, ln_mix_pre[i])
        z = a @ w_in[i]
        q, k, v, zu, zv = jnp.split(z, splits, axis=-1)
        attn = sliding_window_attention(q, k, v, attn_sinks[i])
        gm = chunked_spatial_gating(jax.nn.gelu(zu), jax.nn.gelu(zv),
                                    gm_ln_g[i], gm_ln_b[i], gm_ws[i], gm_bs[i])
        heads = jnp.concatenate([rmsnorm(attn, g_attn_out[i]),
                                 rmsnorm(gm, g_gm_out[i])], axis=-1)
        h = h + rmsnorm(heads @ w_out[i], ln_mix_post[i])
        f = rmsnorm(h, ln_ffn_pre[i])
        f = (jax.nn.silu(f @ w_ffn_gate[i]) * (f @ w_ffn_up[i])) @ w_ffn_down[i]
        h = h + rmsnorm(f, ln_ffn_post[i])
        gate = jax.nn.sigmoid(rmsnorm(h, ln_ple_gate[i]) @ w_ple_gate[i])
        h = h + (p[i] @ w_ple[i]) * gate
    return h
```

```python
import functools

import jax
import jax.numpy as jnp
from jax import lax
from jax.experimental import pallas as pl
from jax.experimental.pallas import tpu as pltpu

HEAD_DIM = 64
N_Q_HEADS = 8
N_KV_HEADS = 2
BLOCK = 128
ATTN_W = N_Q_HEADS * HEAD_DIM
KV_W = N_KV_HEADS * HEAD_DIM
GM_W = 512
NORM_EPS = 1e-6
NEG_BIG = -1e30
LANES = 128

TM_MIX = 512
TM_FFN = 512
VMEM_LIMIT_BYTES = 56 * 1024 * 1024


def _rmsnorm(x, g):
    y = x * lax.rsqrt(jnp.mean(x * x, axis=-1, keepdims=True) + NORM_EPS)
    return y * g


def _dot(a, b):
    return jnp.dot(a, b, preferred_element_type=jnp.float32)


def _dot_nt(a, b):
    return lax.dot_general(a, b, (((1,), (1,)), ((), ())), preferred_element_type=jnp.float32)


def _mix_kernel(sinks_ref, h_ref, g_pre_ref, w_in_ref, ln_g_ref, ln_b_ref, ws_ref, bs_ref,
                g_attn_ref, g_gm_ref, w_out_ref, g_post_ref, o_ref, kv_ref, heads_ref,
                *, tiles_per_seq):
    bf16 = jnp.bfloat16
    f32 = jnp.float32
    tm = h_ref.shape[0]
    n_blk = tm // BLOCK
    first = (pl.program_id(0) % tiles_per_seq) == 0

    @pl.when(first)
    def _():
        kv_ref[0:BLOCK, :] = jnp.zeros((BLOCK, 2 * KV_W), bf16)

    h = h_ref[...]
    a = _rmsnorm(h, g_pre_ref[...]).astype(bf16)
    z = _dot(a, w_in_ref[...])
    kv_ref[BLOCK:, :] = z[:, ATTN_W:ATTN_W + 2 * KV_W].astype(bf16)

    lane = lax.broadcasted_iota(jnp.int32, (1, LANES), 1)
    lo = lane < HEAD_DIM
    qi = lax.broadcasted_iota(jnp.int32, (BLOCK, 2 * BLOCK), 0)
    kj = lax.broadcasted_iota(jnp.int32, (BLOCK, 2 * BLOCK), 1)
    dist = qi + BLOCK - kj
    band = (dist >= 0) & (dist < BLOCK)
    band_first = band & (kj >= BLOCK * first.astype(jnp.int32))
    dist_f = dist.astype(f32)
    scale = HEAD_DIM ** -0.5

    tri = (lax.broadcasted_iota(jnp.int32, (BLOCK, BLOCK), 0)
           >= lax.broadcasted_iota(jnp.int32, (BLOCK, BLOCK), 1))
    w_pairs = []
    for pr in range(N_Q_HEADS // 2):
        w0 = jnp.where(tri, ws_ref[2 * pr], 0.0).astype(bf16)
        w1 = jnp.where(tri, ws_ref[2 * pr + 1], 0.0).astype(bf16)
        w_pairs.append(jnp.concatenate([w0, w1], axis=1))

    for b in range(n_blk):
        r0 = b * BLOCK
        zb = z[r0:r0 + BLOCK]
        valid = band_first if b == 0 else band

        kvb = kv_ref[r0:r0 + 2 * BLOCK, :]
        kb = kvb[:, :KV_W]
        vb = kvb[:, KV_W:]
        kb_sw = jnp.concatenate([kb[:, HEAD_DIM:], kb[:, :HEAD_DIM]], axis=1)
        vb_sw = jnp.concatenate([vb[:, HEAD_DIM:], vb[:, :HEAD_DIM]], axis=1)
        zero = jnp.zeros_like(kb)
        attn_pairs = []
        for kvh in range(N_KV_HEADS):
            k_src, k_src_sw = (kb, kb_sw) if kvh == 0 else (kb_sw, kb)
            v_src, v_src_sw = (vb, vb_sw) if kvh == 0 else (vb_sw, vb)
            k_lo = jnp.where(lo, k_src, zero)
            k_hi = jnp.where(lo, zero, k_src_sw)
            v_lo = jnp.where(lo, v_src, zero)
            v_hi = jnp.where(lo, zero, v_src_sw)
            v_stack = jnp.concatenate([v_lo, v_hi], axis=0)
            p0 = 2 * kvh
            q2 = jnp.concatenate([zb[:, p0 * LANES:(p0 + 1) * LANES],
                                  zb[:, (p0 + 1) * LANES:(p0 + 2) * LANES]], axis=0).astype(bf16)
            s_even = _dot_nt(q2, k_lo)
            s_odd = _dot_nt(q2, k_hi)
            probs = []
            for j in range(2):
                pair = []
                for s_all, par in ((s_even, 0), (s_odd, 1)):
                    hd = 4 * kvh + 2 * j + par
                    s = s_all[j * BLOCK:(j + 1) * BLOCK] * scale
                    slope = 2.0 ** (-8.0 * (hd + 1) / N_Q_HEADS)
                    s = jnp.where(valid, s - slope * dist_f, NEG_BIG)
                    sink = sinks_ref[hd]
                    m = jnp.maximum(jnp.max(s, axis=-1, keepdims=True), sink)
                    e = jnp.exp(s - m)
                    den = jnp.sum(e, axis=-1, keepdims=True) + jnp.exp(sink - m)
                    pair.append((e / den).astype(bf16))
                probs.append(jnp.concatenate(pair, axis=1))
            p_stack = jnp.concatenate(probs, axis=0)
            o2 = _dot(p_stack, v_stack)
            attn_pairs.append(o2[:BLOCK])
            attn_pairs.append(o2[BLOCK:])
        attn = jnp.concatenate(attn_pairs, axis=1)

        zu = zb[:, ATTN_W + 2 * KV_W:ATTN_W + 2 * KV_W + GM_W]
        zv = zb[:, ATTN_W + 2 * KV_W + GM_W:]
        u_act = jax.nn.gelu(zu)
        v_act = jax.nn.gelu(zv)
        mu = jnp.mean(v_act, axis=-1, keepdims=True)
        vc = v_act - mu
        vln = vc * lax.rsqrt(jnp.mean(vc * vc, axis=-1, keepdims=True) + NORM_EPS)
        vln = (vln * ln_g_ref[...] + ln_b_ref[...]).astype(bf16)
        mixed = []
        for pr in range(N_Q_HEADS // 2):
            vp = vln[:, pr * LANES:(pr + 1) * LANES]
            zp = jnp.zeros_like(vp)
            rhs = jnp.concatenate([jnp.where(lo, vp, zp), jnp.where(lo, zp, vp)], axis=0)
            mixed.append(_dot(w_pairs[pr], rhs))
        gm = u_act * (jnp.concatenate(mixed, axis=1) + bs_ref[...])

        heads_ref[r0:r0 + BLOCK, :ATTN_W] = _rmsnorm(attn, g_attn_ref[...]).astype(bf16)
        heads_ref[r0:r0 + BLOCK, ATTN_W:] = _rmsnorm(gm, g_gm_ref[...]).astype(bf16)

    kv_ref[0:BLOCK, :] = kv_ref[tm:tm + BLOCK, :]
    mix = _dot(heads_ref[...], w_out_ref[...])
    o_ref[...] = h + _rmsnorm(mix, g_post_ref[...])


def _ffn_kernel(h_ref, p_ref, g_pre_ref, w_gate_ref, w_up_ref, w_down_ref, g_post_ref,
                w_ple_ref, g_pg_ref, w_pg_ref, o_ref):
    bf16 = jnp.bfloat16
    h = h_ref[...]
    f = _rmsnorm(h, g_pre_ref[...]).astype(bf16)
    gt = _dot(f, w_gate_ref[...])
    up = _dot(f, w_up_ref[...])
    hid = (jax.nn.silu(gt) * up).astype(bf16)
    dn = _dot(hid, w_down_ref[...])
    h = h + _rmsnorm(dn, g_post_ref[...])
    gate = jax.nn.sigmoid(_dot(_rmsnorm(h, g_pg_ref[...]).astype(bf16), w_pg_ref[...]))
    pe = _dot(p_ref[...].astype(bf16), w_ple_ref[...])
    o_ref[...] = h + pe * gate


def _const_spec(shape):
    nd = len(shape)
    return pl.BlockSpec(shape, lambda i, *_: (0,) * nd, pipeline_mode=pl.Buffered(1))


def _mix_call(h, sinks, g_pre, w_in, ln_g, ln_b, ws, bs_full, g_attn, g_gm, w_out, g_post, seq):
    t, d = h.shape
    d_in = w_in.shape[1]
    tm = TM_MIX
    grid_spec = pltpu.PrefetchScalarGridSpec(
        num_scalar_prefetch=1,
        grid=(t // tm,),
        in_specs=[
            pl.BlockSpec((tm, d), lambda i, *_: (i, 0)),
            _const_spec((1, d)),
            _const_spec((d, d_in)),
            _const_spec((1, GM_W)),
            _const_spec((1, GM_W)),
            _const_spec(ws.shape),
            _const_spec(bs_full.shape),
            _const_spec((1, ATTN_W)),
            _const_spec((1, GM_W)),
            _const_spec(w_out.shape),
            _const_spec((1, d)),
        ],
        out_specs=pl.BlockSpec((tm, d), lambda i, *_: (i, 0)),
        scratch_shapes=[
            pltpu.VMEM((tm + BLOCK, 2 * KV_W), jnp.bfloat16),
            pltpu.VMEM((tm, ATTN_W + GM_W), jnp.bfloat16),
        ],
    )
    return pl.pallas_call(
        functools.partial(_mix_kernel, tiles_per_seq=seq // tm),
        grid_spec=grid_spec,
        out_shape=jax.ShapeDtypeStruct((t, d), jnp.float32),
        compiler_params=pltpu.CompilerParams(
            dimension_semantics=("arbitrary",), vmem_limit_bytes=VMEM_LIMIT_BYTES),
    )(sinks, h, g_pre, w_in, ln_g, ln_b, ws, bs_full, g_attn, g_gm, w_out, g_post)


def _ffn_call(h, p, g_pre, w_gate, w_up, w_down, g_post, w_ple, g_pg, w_pg):
    t, d = h.shape
    tm = TM_FFN
    return pl.pallas_call(
        _ffn_kernel,
        grid=(t // tm,),
        in_specs=[
            pl.BlockSpec((tm, d), lambda i: (i, 0)),
            pl.BlockSpec((tm, p.shape[1]), lambda i: (i, 0)),
            _const_spec((1, d)),
            _const_spec(w_gate.shape),
            _const_spec(w_up.shape),
            _const_spec(w_down.shape),
            _const_spec((1, d)),
            _const_spec(w_ple.shape),
            _const_spec((1, d)),
            _const_spec(w_pg.shape),
        ],
        out_specs=pl.BlockSpec((tm, d), lambda i: (i, 0)),
        out_shape=jax.ShapeDtypeStruct((t, d), jnp.float32),
        compiler_params=pltpu.CompilerParams(
            dimension_semantics=("parallel",), vmem_limit_bytes=VMEM_LIMIT_BYTES),
    )(h, p, g_pre, w_gate, w_up, w_down, g_post, w_ple, g_pg, w_pg)


def kernel(x, p, ln_mix_pre, w_in, attn_sinks, gm_ln_g, gm_ln_b, gm_ws, gm_bs, g_attn_out, g_gm_out,
           w_out, ln_mix_post, ln_ffn_pre, w_ffn_gate, w_ffn_up, w_ffn_down, ln_ffn_post, w_ple,
           ln_ple_gate, w_ple_gate):
    batch, seq, d = x.shape
    depth = w_in.shape[0]
    bf16 = jnp.bfloat16
    t = batch * seq
    assert seq % TM_MIX == 0 and t % TM_FFN == 0
    h = x.reshape(t, d)
    p2 = p.reshape(depth, t, p.shape[-1])
    row = lambda v: v.reshape(1, -1)
    for i in range(depth):
        bs_full = jnp.repeat(gm_bs[i].T, HEAD_DIM, axis=1)
        h = _mix_call(h, attn_sinks[i], row(ln_mix_pre[i]), w_in[i].astype(bf16), row(gm_ln_g[i]),
                      row(gm_ln_b[i]), gm_ws[i], bs_full, row(g_attn_out[i]), row(g_gm_out[i]),
                      w_out[i].astype(bf16), row(ln_mix_post[i]), seq)
        h = _ffn_call(h, p2[i], row(ln_ffn_pre[i]), w_ffn_gate[i].astype(bf16),
                      w_ffn_up[i].astype(bf16), w_ffn_down[i].astype(bf16), row(ln_ffn_post[i]),
                      w_ple[i].astype(bf16), row(ln_ple_gate[i]), w_ple_gate[i].astype(bf16))
    return h.reshape(batch, seq, d)
```
